```python
import jax, jax.numpy as jnp
from jax import lax
import numpy as np

D_MODEL = 2048
BATCH = 4
SEQ = 4096
DEPTH = 1
DEC_BATCH = 32
DEC_SEQ = 32
PAST_LEN = 4096

CHUNK = 64
WINDOW = 128
BAND_CHUNKS = WINDOW // CHUNK
D_MIX = D_MODEL
D_ATTN = D_MIX // 2
D_POOL = D_MIX - D_ATTN
HEAD_DIM = 64
N_HEADS = D_ATTN // HEAD_DIM
N_KV_HEADS = N_HEADS // 8
Q_PER_KV = N_HEADS // N_KV_HEADS
KV_DIM = N_KV_HEADS * HEAD_DIM
POOL_WINDOWS = (2, 4, 8, 16)
N_POOL_GROUPS = len(POOL_WINDOWS)
POOL_GROUP_DIM = D_POOL // N_POOL_GROUPS
POOL_HIST = max(POOL_WINDOWS) - 1
D_IN = D_ATTN + 2 * KV_DIM + D_POOL
N_EXPERT_GROUPS = 4
EXPERTS_PER_GROUP = 8
N_EXPERTS = N_EXPERT_GROUPS * EXPERTS_PER_GROUP
TOP_K_IN_GROUP = 2
D_EXPERT = D_MODEL // 4
EPS = 1e-6

kernel_name = "hybrid_swa_sink_pool_hmoe_stream_step"


def rmsnorm(x, w):
    xf = x.astype(jnp.float32)
    y = xf * lax.rsqrt(jnp.mean(xf * xf, axis=-1, keepdims=True) + EPS)
    return (y * w.astype(jnp.float32)).astype(x.dtype)


def project(xn, w_in, q_norm_w, k_norm_w):
    proj = xn @ w_in
    lead = proj.shape[:-1]
    q = proj[..., :D_ATTN].reshape(*lead, N_KV_HEADS, Q_PER_KV, HEAD_DIM)
    k = proj[..., D_ATTN:D_ATTN + KV_DIM].reshape(*lead, N_KV_HEADS, HEAD_DIM)
    v = proj[..., D_ATTN + KV_DIM:D_ATTN + 2 * KV_DIM].reshape(*lead, N_KV_HEADS, HEAD_DIM)
    u = proj[..., D_ATTN + 2 * KV_DIM:]
    return rmsnorm(q, q_norm_w), rmsnorm(k, k_norm_w), v, u


def sink_attention(q, k, v, valid, sinks):
    s = jnp.einsum("...qhgd,...khd->...hgqk", q, k,
                   preferred_element_type=jnp.float32) * (HEAD_DIM ** -0.5)
    s = jnp.where(valid[..., None, None, None, :], s, -jnp.inf)
    sink = sinks.astype(jnp.float32)[:, :, None, None]
    m = jnp.maximum(jnp.max(s, axis=-1, keepdims=True), sink)
    p = jnp.exp(s - m)
    p = p / (jnp.sum(p, axis=-1, keepdims=True) + jnp.exp(sink - m))
    return jnp.einsum("...hgqk,...khd->...qhgd", p.astype(v.dtype), v)


def band_rows(x, n_chunks):
    b = x.shape[0]
    xp = jnp.pad(x, ((0, 0), (BAND_CHUNKS * CHUNK, 0), (0, 0), (0, 0)))
    xc = xp.reshape(b, n_chunks + BAND_CHUNKS, CHUNK, *x.shape[2:])
    return jnp.concatenate([xc[:, i:i + n_chunks] for i in range(BAND_CHUNKS + 1)], axis=2)


def multiscale_pool(u_ext, pos, w_pool, pool_scale):
    t = pos.shape[0]
    uf = u_ext.astype(jnp.float32)
    cs = jnp.pad(jnp.cumsum(uf, axis=1), ((0, 0), (1, 0), (0, 0)))
    end = cs[:, POOL_HIST + 1:]
    u_new = uf[:, POOL_HIST:]
    outs = []
    for g, w in enumerate(POOL_WINDOWS):
        sl = slice(g * POOL_GROUP_DIM, (g + 1) * POOL_GROUP_DIM)
        start = cs[:, POOL_HIST + 1 - w:POOL_HIST + 1 - w + t, sl]
        cnt = jnp.minimum(pos + 1, w).astype(jnp.float32)[:, None]
        d = ((end[..., sl] - start) / cnt - u_new[..., sl]).astype(u_ext.dtype)
        outs.append(d @ w_pool[g])
    return jnp.concatenate(outs, axis=-1) * pool_scale


def hier_moe(x2d, w_group_router, w_expert_router, w_gate, w_up, w_down):
    n = x2d.shape[0]
    g_prob = jax.nn.softmax((x2d @ w_group_router).astype(jnp.float32), axis=-1)
    g_top, g_idx = lax.top_k(g_prob, 1)
    e_logits = (x2d @ w_expert_router).astype(jnp.float32).reshape(n, N_EXPERT_GROUPS, EXPERTS_PER_GROUP)
    idx = jnp.broadcast_to(g_idx[:, :, None], (n, 1, EXPERTS_PER_GROUP))
    e_logits = jnp.take_along_axis(e_logits, idx, axis=1)[:, 0]
    e_top, e_idx = lax.top_k(jax.nn.softmax(e_logits, axis=-1), TOP_K_IN_GROUP)
    gate = g_top * (e_top / jnp.sum(e_top, axis=-1, keepdims=True))
    expert_id = g_idx * EXPERTS_PER_GROUP + e_idx
    combine = jnp.sum(jax.nn.one_hot(expert_id, N_EXPERTS, dtype=jnp.float32) * gate[..., None], axis=1)
    out = jnp.zeros(x2d.shape, jnp.float32)
    for e in range(N_EXPERTS):
        h = jax.nn.silu(x2d @ w_gate[e]) * (x2d @ w_up[e])
        out = out + combine[:, e:e + 1] * (h @ w_down[e]).astype(jnp.float32)
    return out.astype(x2d.dtype)


def setup_inputs(seed: int = 0) -> dict:
    key = jax.random.key(seed)
    ks = jax.random.split(key, 20)
    f32 = jnp.float32
    cache_rows = min(WINDOW, PAST_LEN)
    nrm = lambda k, shape, scale: jax.random.normal(k, shape, f32) * scale
    return {
        "x_prompt": nrm(ks[0], (BATCH, SEQ, D_MODEL), 1.0),
        "x_sample": nrm(ks[1], (DEC_BATCH, DEC_SEQ, D_MODEL), 1.0),
        "cache_k": nrm(ks[2], (DEPTH, DEC_BATCH, cache_rows, N_KV_HEADS, HEAD_DIM), 1.0),
        "cache_v": nrm(ks[3], (DEPTH, DEC_BATCH, cache_rows, N_KV_HEADS, HEAD_DIM), 1.0),
        "state_pool": nrm(ks[4], (DEPTH, DEC_BATCH, POOL_HIST, D_POOL), 1.0),
        "attn_norm_w": 1.0 + nrm(ks[5], (DEPTH, D_MODEL), 0.1),
        "w_in": nrm(ks[6], (DEPTH, D_MODEL, D_IN), D_MODEL ** -0.5),
        "q_norm_w": 1.0 + nrm(ks[7], (DEPTH, HEAD_DIM), 0.1),
        "k_norm_w": 1.0 + nrm(ks[8], (DEPTH, HEAD_DIM), 0.1),
        "attn_sinks": nrm(ks[9], (DEPTH, N_HEADS), 1.0),
        "w_pool": nrm(ks[10], (DEPTH, N_POOL_GROUPS, POOL_GROUP_DIM, POOL_GROUP_DIM), POOL_GROUP_DIM ** -0.5),
        "pool_scale": 1.0 + nrm(ks[11], (DEPTH, D_POOL), 0.1),
        "w_out": nrm(ks[12], (DEPTH, D_MIX, D_MODEL), D_MIX ** -0.5),
        "ffn_norm_w": 1.0 + nrm(ks[13], (DEPTH, D_MODEL), 0.1),
        "w_group_router": nrm(ks[14], (DEPTH, D_MODEL, N_EXPERT_GROUPS), D_MODEL ** -0.5),
        "w_expert_router": nrm(ks[15], (DEPTH, D_MODEL, N_EXPERTS), D_MODEL ** -0.5),
        "w_gate": nrm(ks[16], (DEPTH, N_EXPERTS, D_MODEL, D_EXPERT), D_MODEL ** -0.5),
        "w_up": nrm(ks[17], (DEPTH, N_EXPERTS, D_MODEL, D_EXPERT), D_MODEL ** -0.5),
        "w_down": nrm(ks[18], (DEPTH, N_EXPERTS, D_EXPERT, D_MODEL), D_EXPERT ** -0.5),
    }


def reference(x_prompt, x_sample, cache_k, cache_v, state_pool, attn_norm_w, w_in, q_norm_w,
              k_norm_w, attn_sinks, w_pool, pool_scale, w_out, ffn_norm_w, w_group_router,
              w_expert_router, w_gate, w_up, w_down):
    b, s, d = x_prompt.shape
    db, t, _ = x_sample.shape
    n_chunks = s // CHUNK
    band = (BAND_CHUNKS + 1) * CHUNK
    band_pos = (jnp.arange(n_chunks)[:, None] - BAND_CHUNKS) * CHUNK + jnp.arange(band)[None, :]
    band_valid = band_pos >= 0
    pos_prompt = jnp.arange(s)
    pos_sample = PAST_LEN + jnp.arange(t)

    h_p, h_s = x_prompt, x_sample
    nk_p, nv_p, np_p, nk_s, nv_s, np_s = [], [], [], [], [], []
    for l in range(DEPTH):
        sinks = attn_sinks[l].reshape(N_KV_HEADS, Q_PER_KV)

        q, k, v, u = project(rmsnorm(h_p, attn_norm_w[l]), w_in[l], q_norm_w[l], k_norm_w[l])
        qc = q.reshape(b, n_chunks, CHUNK, N_KV_HEADS, Q_PER_KV, HEAD_DIM)
        a = sink_attention(qc, band_rows(k, n_chunks), band_rows(v, n_chunks), band_valid, sinks)
        a = a.reshape(b, s, D_ATTN)
        u_ext = jnp.pad(u, ((0, 0), (POOL_HIST, 0), (0, 0)))
        p = multiscale_pool(u_ext, pos_prompt, w_pool[l], pool_scale[l])
        h_p = h_p + jnp.concatenate([a, p], axis=-1) @ w_out[l]
        xn = rmsnorm(h_p, ffn_norm_w[l]).reshape(b * s, d)
        h_p = h_p + hier_moe(xn, w_group_router[l], w_expert_router[l], w_gate[l], w_up[l], w_down[l]).reshape(b, s, d)
        nk_p.append(k[:, s - WINDOW:])
        nv_p.append(v[:, s - WINDOW:])
        np_p.append(u[:, s - POOL_HIST:])

        q_s, k_s, v_s, u_s = project(rmsnorm(h_s, attn_norm_w[l]), w_in[l], q_norm_w[l], k_norm_w[l])
        k_all = jnp.concatenate([cache_k[l], k_s], axis=1)
        v_all = jnp.concatenate([cache_v[l], v_s], axis=1)
        valid_s = jnp.ones((k_all.shape[1],), dtype=bool)
        a_s = sink_attention(q_s, k_all, v_all, valid_s, sinks).reshape(db, t, D_ATTN)
        u_ext_s = jnp.concatenate([state_pool[l], u_s], axis=1)
        p_s = multiscale_pool(u_ext_s, pos_sample, w_pool[l], pool_scale[l])
        h_s = h_s + jnp.concatenate([a_s, p_s], axis=-1) @ w_out[l]
        xn_s = rmsnorm(h_s, ffn_norm_w[l]).reshape(db * t, d)
        h_s = h_s + hier_moe(xn_s, w_group_router[l], w_expert_router[l], w_gate[l], w_up[l], w_down[l]).reshape(db, t, d)
        nk_s.append(k_all[:, t:])
        nv_s.append(v_all[:, t:])
        np_s.append(u_ext_s[:, t:])

    return (h_p, h_s, jnp.stack(nk_p, 0), jnp.stack(nv_p, 0), jnp.stack(np_p, 0),
            jnp.stack(nk_s, 0), jnp.stack(nv_s, 0), jnp.stack(np_s, 0))
```

```python
import functools

import jax
import jax.numpy as jnp
from jax import lax
from jax.experimental import pallas as pl
from jax.experimental.pallas import tpu as pltpu

CHUNK = 64
WINDOW = 128
HEAD_DIM = 64
N_HEADS = 16
N_KV_HEADS = 2
Q_PER_KV = N_HEADS // N_KV_HEADS
D_ATTN = N_HEADS * HEAD_DIM
KV_DIM = N_KV_HEADS * HEAD_DIM
POOL_WINDOWS = (2, 4, 8, 16)
POOL_GROUP_DIM = 256
D_POOL = POOL_GROUP_DIM * len(POOL_WINDOWS)
POOL_HIST = max(POOL_WINDOWS) - 1
HIST_ROWS = POOL_HIST + 1
N_EXPERT_GROUPS = 4
EXPERTS_PER_GROUP = 8
N_EXPERTS = N_EXPERT_GROUPS * EXPERTS_PER_GROUP
D_EXPERT = 512
EPS = 1e-6
PAST_LEN = 4096
ROUTER_ROWS = 8 + N_EXPERTS
LANES = 128
MASKED = -1e30
VMEM_LIMIT = 56 * 1024 * 1024

BF16 = jnp.bfloat16
F32 = jnp.float32
NT_DIMS = (((1,), (1,)), ((), ()))


def _params(sem):
    return pltpu.CompilerParams(dimension_semantics=sem, vmem_limit_bytes=VMEM_LIMIT)


def _inproj_kernel(x_ref, hist_ref, nw_ref, wqkv_ref, wu_ref, qnw_ref, knw_ref, wpool_ref, pscale_ref,
                   qT_ref, k_ref, v_ref, p_ref, utail_ref, uext_ref, *, seqs, rows, pos0):
    t = pl.program_id(1)
    n = seqs * rows
    x = x_ref[...]
    ms = jnp.mean(x * x, axis=-1, keepdims=True)
    xn = (x * lax.rsqrt(ms + EPS) * nw_ref[...]).astype(BF16)

    qkv = lax.dot_general(wqkv_ref[...], xn, NT_DIMS, preferred_element_type=F32)
    q = qkv[:D_ATTN].reshape(N_HEADS, HEAD_DIM, n)
    q = q * lax.rsqrt(jnp.mean(q * q, axis=1, keepdims=True) + EPS) * qnw_ref[...][None]
    qT_ref[...] = (q * (HEAD_DIM ** -0.5)).reshape(D_ATTN, n).astype(BF16)
    k = qkv[D_ATTN:D_ATTN + KV_DIM].reshape(N_KV_HEADS, HEAD_DIM, n)
    k = k * lax.rsqrt(jnp.mean(k * k, axis=1, keepdims=True) + EPS) * knw_ref[...][None]
    k_ref[...] = k.reshape(KV_DIM, n).T
    v_ref[...] = qkv[D_ATTN + KV_DIM:].T

    u = jnp.dot(xn, wu_ref[...], preferred_element_type=F32)

    @pl.when(t == 0)
    def _():
        uext_ref[:, :HIST_ROWS, :] = hist_ref[...]

    @pl.when(t > 0)
    def _():
        uext_ref[:, :HIST_ROWS, :] = uext_ref[:, rows:rows + HIST_ROWS, :]

    uext_ref[:, HIST_ROWS:, :] = u.reshape(seqs, rows, D_POOL)
    utail_ref[...] = uext_ref[:, rows:rows + HIST_ROWS, :]

    pos = pos0 + t * rows + lax.broadcasted_iota(jnp.int32, (1, rows, 1), 1)
    outs = []
    for g, w in enumerate(POOL_WINDOWS):
        c0 = g * POOL_GROUP_DIM
        acc = uext_ref[:, HIST_ROWS:, c0:c0 + POOL_GROUP_DIM]
        u_new = acc
        for j in range(1, w):
            acc = acc + uext_ref[:, HIST_ROWS - j:HIST_ROWS - j + rows, c0:c0 + POOL_GROUP_DIM]
        inv_cnt = 1.0 / jnp.minimum(pos + 1, w).astype(F32)
        d = (acc * inv_cnt - u_new).astype(BF16).reshape(n, POOL_GROUP_DIM)
        outs.append(jnp.dot(d, wpool_ref[g], preferred_element_type=F32))
    p_ref[...] = (jnp.concatenate(outs, axis=-1) * pscale_ref[...]).astype(BF16)


def _inproj(x2d, hist, nw, wqkvT, wu, qnw, knw, wpool, pscale, *, seqs, rows, tiles_per_seq, pos0):
    n_tok, d = x2d.shape
    n = seqs * rows
    groups = n_tok // (n * tiles_per_seq)
    const = lambda *shape: pl.BlockSpec(shape, lambda g, t: (0,) * len(shape))
    return pl.pallas_call(
        functools.partial(_inproj_kernel, seqs=seqs, rows=rows, pos0=pos0),
        grid=(groups, tiles_per_seq),
        in_specs=[
            pl.BlockSpec((n, d), lambda g, t: (g * tiles_per_seq + t, 0)),
            pl.BlockSpec((seqs, HIST_ROWS, D_POOL), lambda g, t: (g, 0, 0)),
            const(1, d), const(D_ATTN + 2 * KV_DIM, d), const(d, D_POOL), const(HEAD_DIM, 1), const(HEAD_DIM, 1),
            const(len(POOL_WINDOWS), POOL_GROUP_DIM, POOL_GROUP_DIM), const(1, D_POOL),
        ],
        out_specs=[
            pl.BlockSpec((D_ATTN, n), lambda g, t: (0, g * tiles_per_seq + t)),
            pl.BlockSpec((n, KV_DIM), lambda g, t: (g * tiles_per_seq + t, 0)),
            pl.BlockSpec((n, KV_DIM), lambda g, t: (g * tiles_per_seq + t, 0)),
            pl.BlockSpec((n, D_POOL), lambda g, t: (g * tiles_per_seq + t, 0)),
            pl.BlockSpec((seqs, HIST_ROWS, D_POOL), lambda g, t: (g, 0, 0)),
        ],
        out_shape=[
            jax.ShapeDtypeStruct((D_ATTN, n_tok), BF16),
            jax.ShapeDtypeStruct((n_tok, KV_DIM), F32),
            jax.ShapeDtypeStruct((n_tok, KV_DIM), F32),
            jax.ShapeDtypeStruct((n_tok, D_POOL), BF16),
            jax.ShapeDtypeStruct((groups * seqs, HIST_ROWS, D_POOL), F32),
        ],
        scratch_shapes=[pltpu.VMEM((seqs, HIST_ROWS + rows, D_POOL), F32)],
        compiler_params=_params(("arbitrary", "arbitrary")),
        name="inproj",
    )(x2d, hist, nw, wqkvT, wu, qnw, knw, wpool, pscale)


def _attn_core(qT_ref, k, v, mask, sinks_ref, o_ref):
    k = k.astype(BF16)
    vT = v.T.astype(BF16)
    zeros = jnp.zeros((HEAD_DIM, LANES), BF16)
    for pair in range(N_HEADS // 2):
        outs = []
        for h in (2 * pair, 2 * pair + 1):
            j = h // Q_PER_KV
            qh = qT_ref[h * HEAD_DIM:(h + 1) * HEAD_DIM, :]
            qpad = jnp.concatenate([qh, zeros] if j == 0 else [zeros, qh], axis=0)
            s = jnp.dot(k, qpad, preferred_element_type=F32)
            s = jnp.where(mask, s, MASKED)
            sink = sinks_ref[h]
            m = jnp.maximum(jnp.max(s, axis=0, keepdims=True), sink)
            p = jnp.exp(s - m)
            denom = jnp.sum(p, axis=0, keepdims=True) + jnp.exp(sink - m)
            o = jnp.dot(vT[j * HEAD_DIM:(j + 1) * HEAD_DIM, :], p.astype(BF16), preferred_element_type=F32)
            outs.append(o / denom)
        o_ref[:, pair * LANES:(pair + 1) * LANES] = jnp.concatenate(outs, axis=0).T.astype(BF16)


def _attn_prompt_kernel(sinks_ref, qT_ref, kp_ref, kc_ref, vp_ref, vc_ref, o_ref):
    i = pl.program_id(1)
    k = jnp.concatenate([kp_ref[...], kc_ref[...]], axis=0)
    v = jnp.concatenate([vp_ref[...], vc_ref[...]], axis=0)
    key = lax.broadcasted_iota(jnp.int32, (2 * LANES, LANES), 0)
    tok = lax.broadcasted_iota(jnp.int32, (2 * LANES, LANES), 1)
    kc, tc = key // CHUNK, tok // CHUNK
    first_key = jnp.where(i > 0, 0, LANES)
    mask = (kc >= tc) & (kc <= tc + WINDOW // CHUNK) & (key >= first_key)
    _attn_core(qT_ref, k, v, mask, sinks_ref, o_ref)


def _attn_prompt(sinks, qT, k, v, *, batch):
    n_tok = qT.shape[1]
    nb = n_tok // batch // LANES
    cur = lambda b, i: (b * nb + i, 0)
    prev = lambda b, i: (b * nb + jnp.maximum(i - 1, 0), 0)
    return pl.pallas_call(
        _attn_prompt_kernel,
        grid=(batch, nb),
        in_specs=[
            pl.BlockSpec(memory_space=pltpu.SMEM),
            pl.BlockSpec((D_ATTN, LANES), lambda b, i: (0, b * nb + i)),
            pl.BlockSpec((LANES, KV_DIM), prev), pl.BlockSpec((LANES, KV_DIM), cur),
            pl.BlockSpec((LANES, KV_DIM), prev), pl.BlockSpec((LANES, KV_DIM), cur),
        ],
        out_specs=pl.BlockSpec((LANES, D_ATTN), cur),
        out_shape=jax.ShapeDtypeStruct((n_tok, D_ATTN), BF16),
        compiler_params=_params(("arbitrary", "arbitrary")),
        name="attn_prompt",
    )(sinks, qT, k, k, v, v)


def _attn_sample_kernel(sinks_ref, qT_ref, ck_ref, kn_ref, cv_ref, vn_ref, o_ref, *, steps, cache_rows):
    seqs = LANES // steps
    k = jnp.concatenate([ck_ref[...], kn_ref[...]], axis=0)
    v = jnp.concatenate([cv_ref[...], vn_ref[...]], axis=0)
    n_cache = seqs * cache_rows
    key = lax.broadcasted_iota(jnp.int32, (n_cache + LANES, LANES), 0)
    tok = lax.broadcasted_iota(jnp.int32, (n_cache + LANES, LANES), 1)
    key_seq = jnp.where(key < n_cache, key // cache_rows, (key - n_cache) // steps)
    _attn_core(qT_ref, k, v, key_seq == tok // steps, sinks_ref, o_ref)


def _attn_sample(sinks, qT, cache_k, k_new, cache_v, v_new, *, steps, cache_rows):
    n_tok = qT.shape[1]
    seqs = LANES // steps
    blk = lambda i: (i, 0)
    return pl.pallas_call(
        functools.partial(_attn_sample_kernel, steps=steps, cache_rows=cache_rows),
        grid=(n_tok // LANES,),
        in_specs=[
            pl.BlockSpec(memory_space=pltpu.SMEM),
            pl.BlockSpec((D_ATTN, LANES), lambda i: (0, i)),
            pl.BlockSpec((seqs * cache_rows, KV_DIM), blk), pl.BlockSpec((LANES, KV_DIM), blk),
            pl.BlockSpec((seqs * cache_rows, KV_DIM), blk), pl.BlockSpec((LANES, KV_DIM), blk),
        ],
        out_specs=pl.BlockSpec((LANES, D_ATTN), blk),
        out_shape=jax.ShapeDtypeStruct((n_tok, D_ATTN), BF16),
        compiler_params=_params(("arbitrary",)),
        name="attn_sample",
    )(sinks, qT, cache_k, k_new, cache_v, v_new)


def _first_index_of_max(x, n):
    top = jnp.max(x, axis=0, keepdims=True)
    rows = lax.broadcasted_iota(jnp.int32, x.shape, 0)
    return top, jnp.min(jnp.where(x == top, rows, n), axis=0, keepdims=True)


def _softmax_rows(x):
    e = jnp.exp(x - jnp.max(x, axis=0, keepdims=True))
    return e / jnp.sum(e, axis=0, keepdims=True)


def _outproj_kernel(a_ref, p_ref, x_ref, wout_ref, fnw_ref, wr2_ref, wr1_ref, cnt_in_ref,
                    h_ref, xp_ref, ri_ref, gcol_ref, cnt_ref, carry_ref):
    i = pl.program_id(0)
    t = x_ref.shape[0]
    d = x_ref.shape[1]

    @pl.when(i == 0)
    def _():
        carry_ref[...] = cnt_in_ref[...]

    h = (x_ref[...] + jnp.dot(a_ref[...], wout_ref[:D_ATTN, :], preferred_element_type=F32)
         + jnp.dot(p_ref[...], wout_ref[D_ATTN:, :], preferred_element_type=F32))
    h_ref[...] = h
    xn = h * lax.rsqrt(jnp.mean(h * h, axis=-1, keepdims=True) + EPS) * fnw_ref[...]
    xh = xn.astype(BF16)
    xhf = xh.astype(F32)

    bits = lax.bitcast_convert_type(xhf, jnp.uint32)
    xp_ref[...] = (bits[:, :d // 2] >> 16) | bits[:, d // 2:]

    xl = (xn - xhf).astype(BF16)
    lg2 = lax.dot_general(wr2_ref[...], xh, NT_DIMS, preferred_element_type=F32)
    lg1 = lax.dot_general(wr1_ref[...], xl, NT_DIMS, preferred_element_type=F32)
    lg = lg2[:ROUTER_ROWS] + lg2[ROUTER_ROWS:] + lg1

    rows8 = lax.broadcasted_iota(jnp.int32, (8, t), 0)
    g_logits = jnp.where(rows8 < N_EXPERT_GROUPS, lg[:8], MASKED)
    g_top, g_idx = _first_index_of_max(_softmax_rows(g_logits), N_EXPERT_GROUPS)
    e_sel = lg[8:8 + EXPERTS_PER_GROUP]
    for g in range(1, N_EXPERT_GROUPS):
        e_sel = jnp.where(g_idx == g, lg[8 + g * EXPERTS_PER_GROUP:8 + (g + 1) * EXPERTS_PER_GROUP], e_sel)
    e_prob = _softmax_rows(e_sel)
    t1, i1 = _first_index_of_max(e_prob, EXPERTS_PER_GROUP)
    t2, i2 = _first_index_of_max(jnp.where(rows8 == i1, -1.0, e_prob), EXPERTS_PER_GROUP)
    gate1 = g_top * (t1 / (t1 + t2))
    gate2 = g_top * (t2 / (t1 + t2))
    eid1 = g_idx * EXPERTS_PER_GROUP + i1
    eid2 = g_idx * EXPERTS_PER_GROUP + i2

    erow = lax.broadcasted_iota(jnp.int32, (N_EXPERTS, t), 0)
    oh1 = erow == eid1
    oh2 = erow == eid2
    oh = (oh1 | oh2).astype(F32)
    src = lax.broadcasted_iota(jnp.int32, (t, t), 0)
    dst = lax.broadcasted_iota(jnp.int32, (t, t), 1)
    before = jnp.dot(oh.astype(BF16), (src < dst).astype(BF16), preferred_element_type=F32) + carry_ref[:, :1]
    rank1 = jnp.sum(jnp.where(oh1, before, 0.0), axis=0, keepdims=True).astype(jnp.int32)
    rank2 = jnp.sum(jnp.where(oh2, before, 0.0), axis=0, keepdims=True).astype(jnp.int32)
    carry_ref[...] = carry_ref[...] + jnp.sum(oh, axis=1, keepdims=True)
    cnt_ref[...] = carry_ref[...]

    r8 = lax.broadcasted_iota(jnp.int32, (8, t), 0)
    ri_ref[...] = jnp.where(r8 == 0, eid1, jnp.where(r8 == 1, eid2, jnp.where(r8 == 2, rank1, jnp.where(r8 == 3, rank2, 0))))
    r128 = lax.broadcasted_iota(jnp.int32, (LANES, t), 0)
    gcol_ref[...] = jnp.where(r128 == 0, gate1, jnp.where(r128 == 1, gate2, 0.0)).T


def _outproj(a, p, x2d, wout, fnw, wr2, wr1, cnt_in, *, tile):
    n_tok, d = x2d.shape
    const = lambda *shape: pl.BlockSpec(shape, lambda i: (0,) * len(shape))
    row = lambda width: pl.BlockSpec((tile, width), lambda i: (i, 0))
    return pl.pallas_call(
        _outproj_kernel,
        grid=(n_tok // tile,),
        in_specs=[row(D_ATTN), row(D_POOL), row(d), const(D_ATTN + D_POOL, d), const(1, d),
                  const(2 * ROUTER_ROWS, d), const(ROUTER_ROWS, d), const(N_EXPERTS, LANES)],
        out_specs=[row(d), row(d // 2), pl.BlockSpec((8, tile), lambda i: (0, i)), row(LANES),
                   const(N_EXPERTS, LANES)],
        out_shape=[
            jax.ShapeDtypeStruct((n_tok, d), F32),
            jax.ShapeDtypeStruct((n_tok, d // 2), jnp.uint32),
            jax.ShapeDtypeStruct((8, n_tok), jnp.int32),
            jax.ShapeDtypeStruct((n_tok, LANES), F32),
            jax.ShapeDtypeStruct((N_EXPERTS, LANES), F32),
        ],
        scratch_shapes=[pltpu.VMEM((N_EXPERTS, LANES), F32)],
        compiler_params=_params(("arbitrary",)),
        name="outproj",
    )(a, p, x2d, wout, fnw, wr2, wr1, cnt_in)


def _row_copy(src_ref, src_row, dst_ref, dst_row, sem):
    return pltpu.make_async_copy(src_ref.at[pl.ds(src_row, 1)], dst_ref.at[pl.ds(dst_row, 1)], sem)


def _dispatch_kernel(dest_ref, x_ref, xs_in_ref, xs_ref, sem):
    del xs_in_ref
    t = x_ref.shape[0]

    def start(r, c):
        for s in range(2):
            _row_copy(x_ref, r, xs_ref, dest_ref[s, r], sem).start()
        return c

    def wait(r, c):
        for s in range(2):
            _row_copy(x_ref, 0, xs_ref, 0, sem).wait()
        return c

    lax.fori_loop(0, t, start, 0)
    lax.fori_loop(0, t, wait, 0)


def _dispatch(dest, xp, xs_prev, *, tile):
    n_tok, width = xp.shape
    return pl.pallas_call(
        _dispatch_kernel,
        grid=(n_tok // tile,),
        in_specs=[
            pl.BlockSpec((None, 2, tile), lambda i: (i, 0, 0), memory_space=pltpu.SMEM),
            pl.BlockSpec((tile, width), lambda i: (i, 0)),
            pl.BlockSpec(memory_space=pl.ANY),
        ],
        out_specs=pl.BlockSpec(memory_space=pl.ANY),
        out_shape=jax.ShapeDtypeStruct(xs_prev.shape, xs_prev.dtype),
        scratch_shapes=[pltpu.SemaphoreType.DMA(())],
        input_output_aliases={2: 0},
        compiler_params=_params(("arbitrary",)),
        name="dispatch",
    )(dest, xp, xs_prev)


def _experts_kernel(te_ref, tv_ref, x_ref, wg_ref, wu_ref, wd_ref, y_ref, wgu_bf, wd_bf):
    i = pl.program_id(0)
    valid = tv_ref[i]
    tm = x_ref.shape[0]
    half = x_ref.shape[1]

    @pl.when(valid == 0)
    def _():
        y_ref[...] = jnp.zeros_like(y_ref)

    @pl.when(valid > 0)
    def _():
        @pl.when((i == 0) | (te_ref[jnp.maximum(i - 1, 0)] != te_ref[i]))
        def _():
            wgu_bf[:, :D_EXPERT] = wg_ref[...].astype(BF16)
            wgu_bf[:, D_EXPERT:] = wu_ref[...].astype(BF16)
            wd_bf[...] = wd_ref[...].astype(BF16)

        rows = lax.broadcasted_iota(jnp.int32, (tm, 1), 0)
        xp = jnp.where(rows < valid, x_ref[...], jnp.uint32(0))
        x_lo = lax.bitcast_convert_type(xp << 16, F32).astype(BF16)
        x_hi = lax.bitcast_convert_type(xp & jnp.uint32(0xFFFF0000), F32).astype(BF16)
        gu = (jnp.dot(x_lo, wgu_bf[:half, :], preferred_element_type=F32)
              + jnp.dot(x_hi, wgu_bf[half:, :], preferred_element_type=F32))
        g = gu[:, :D_EXPERT]
        hid = g * (1.0 / (1.0 + jnp.exp(-g))) * gu[:, D_EXPERT:]
        y_ref[...] = jnp.dot(hid.astype(BF16), wd_bf[...], preferred_element_type=F32)


def _experts(tile_expert, tile_valid, xs, w_gate, w_up, w_down, *, tm):
    rows_sorted, half = xs.shape
    d = 2 * half
    wsel = lambda i, te, tv: (te[i], 0, 0)
    return pl.pallas_call(
        _experts_kernel,
        grid_spec=pltpu.PrefetchScalarGridSpec(
            num_scalar_prefetch=2,
            grid=(rows_sorted // tm,),
            in_specs=[
                pl.BlockSpec((tm, half), lambda i, te, tv: (i, 0)),
                pl.BlockSpec((None, d, D_EXPERT), wsel),
                pl.BlockSpec((None, d, D_EXPERT), wsel),
                pl.BlockSpec((None, D_EXPERT, d), wsel),
            ],
            out_specs=pl.BlockSpec((tm, d), lambda i, te, tv: (i, 0)),
            scratch_shapes=[pltpu.VMEM((d, 2 * D_EXPERT), BF16), pltpu.VMEM((D_EXPERT, d), BF16)],
        ),
        out_shape=jax.ShapeDtypeStruct((rows_sorted, d), F32),
        compiler_params=_params(("arbitrary",)),
        name="experts",
    )(tile_expert, tile_valid, xs, w_gate, w_up, w_down)


def _combine_kernel(dest_ref, h_ref, gcol_ref, ys_ref, y_ref, buf_ref, sem):
    t = h_ref.shape[0]

    def start(r, c):
        for s in range(2):
            _row_copy(ys_ref, dest_ref[s, r], buf_ref.at[s], r, sem).start()
        return c

    def wait(r, c):
        for s in range(2):
            _row_copy(ys_ref, 0, buf_ref.at[s], 0, sem).wait()
        return c

    lax.fori_loop(0, t, start, 0)
    lax.fori_loop(0, t, wait, 0)
    gates = gcol_ref[...]
    y_ref[...] = h_ref[...] + gates[:, 0:1] * buf_ref[0] + gates[:, 1:2] * buf_ref[1]


def _combine(dest, h, gcol, ys, *, tile):
    n_tok, d = h.shape
    return pl.pallas_call(
        _combine_kernel,
        grid=(n_tok // tile,),
        in_specs=[
            pl.BlockSpec((None, 2, tile), lambda i: (i, 0, 0), memory_space=pltpu.SMEM),
            pl.BlockSpec((tile, d), lambda i: (i, 0)),
            pl.BlockSpec((tile, LANES), lambda i: (i, 0)),
            pl.BlockSpec(memory_space=pl.ANY),
        ],
        out_specs=pl.BlockSpec((tile, d), lambda i: (i, 0)),
        out_shape=jax.ShapeDtypeStruct((n_tok, d), F32),
        scratch_shapes=[pltpu.VMEM((2, tile, d), F32), pltpu.SemaphoreType.DMA(())],
        compiler_params=_params(("arbitrary",)),
        name="combine",
    )(dest, h, gcol, ys)


def _largest_tile(n, cap):
    t = cap
    while n % t:
        t //= 2
    return t


def _dest_tiles(ri, offsets, tile):
    dest = offsets[ri[0:2]] + ri[2:4]
    return dest.reshape(2, -1, tile).transpose(1, 0, 2)


def _layer(x_p, x_s, cache_k, cache_v, state_pool, attn_norm_w, w_in, q_norm_w, k_norm_w, sinks, w_pool,
           pool_scale, w_out, ffn_norm_w, w_gr, w_er, w_gate, w_up, w_down):
    b, s, d = x_p.shape
    db, steps, _ = x_s.shape
    cache_rows = cache_k.shape[1]
    n_p, n_s = b * s, db * steps

    wqkvT = w_in[:, :D_ATTN + 2 * KV_DIM].T.astype(BF16)
    wu = w_in[:, D_ATTN + 2 * KV_DIM:].astype(BF16)
    nw = attn_norm_w.reshape(1, d)
    qnw = q_norm_w.reshape(HEAD_DIM, 1)
    knw = k_norm_w.reshape(HEAD_DIM, 1)
    wpool = w_pool.astype(BF16)
    pscale = pool_scale.reshape(1, D_POOL)
    wout = w_out.astype(BF16)
    fnw = ffn_norm_w.reshape(1, d)
    wrT = jnp.concatenate([w_gr.T, jnp.zeros((8 - N_EXPERT_GROUPS, d), F32), w_er.T], axis=0)
    wr_hi = wrT.astype(BF16)
    wr_lo = (wrT - wr_hi.astype(F32)).astype(BF16)
    wr2 = jnp.concatenate([wr_hi, wr_lo], axis=0)

    xp2, xs2 = x_p.reshape(n_p, d), x_s.reshape(n_s, d)

    rows_p = _largest_tile(s, 512)
    qT_p, k_p, v_p, p_p, ut_p = _inproj(
        xp2, jnp.zeros((b, HIST_ROWS, D_POOL), F32), nw, wqkvT, wu, qnw, knw, wpool, pscale,
        seqs=1, rows=rows_p, tiles_per_seq=s // rows_p, pos0=0)
    seqs_s = _largest_tile(db, max(1, 256 // steps))
    hist_s = jnp.pad(state_pool, ((0, 0), (HIST_ROWS - POOL_HIST, 0), (0, 0)))
    qT_s, k_s, v_s, p_s, ut_s = _inproj(
        xs2, hist_s, nw, wqkvT, wu, qnw, knw, wpool, pscale,
        seqs=seqs_s, rows=steps, tiles_per_seq=1, pos0=PAST_LEN)

    a_p = _attn_prompt(sinks, qT_p, k_p, v_p, batch=b)
    ck2 = cache_k.reshape(db * cache_rows, KV_DIM)
    cv2 = cache_v.reshape(db * cache_rows, KV_DIM)
    a_s = _attn_sample(sinks, qT_s, ck2, k_s, cv2, v_s, steps=steps, cache_rows=cache_rows)

    tile_p, tile_s = _largest_tile(n_p, 512), _largest_tile(n_s, 512)
    h_p, xpk_p, ri_p, gcol_p, cnt_p = _outproj(a_p, p_p, xp2, wout, fnw, wr2, wr_hi,
                                               jnp.zeros((N_EXPERTS, LANES), F32), tile=tile_p)
    h_s, xpk_s, ri_s, gcol_s, cnt = _outproj(a_s, p_s, xs2, wout, fnw, wr2, wr_hi, cnt_p, tile=tile_s)

    tm = 256
    n_tiles = (2 * (n_p + n_s)) // tm + N_EXPERTS
    counts = cnt[:, 0].astype(jnp.int32)
    tiles_per_expert = (counts + tm - 1) // tm
    tile_end = jnp.cumsum(tiles_per_expert)
    offsets = (tile_end - tiles_per_expert) * tm
    tile_ids = jnp.arange(n_tiles, dtype=jnp.int32)
    tile_expert = jnp.sum((tile_end[None, :] <= tile_ids[:, None]).astype(jnp.int32), axis=1)
    tile_expert = jnp.minimum(tile_expert, N_EXPERTS - 1)
    tile_valid = jnp.clip(counts[tile_expert] - (tile_ids * tm - offsets[tile_expert]), 0, tm)
    tile_valid = jnp.where(tile_ids < tile_end[-1], tile_valid, 0).astype(jnp.int32)
    tile_expert = jnp.where(tile_ids < tile_end[-1], tile_expert, tile_expert[jnp.maximum(tile_end[-1] - 1, 0)])
    dest_p, dest_s = _dest_tiles(ri_p, offsets, tile_p), _dest_tiles(ri_s, offsets, tile_s)

    xs_sorted = jnp.zeros((n_tiles * tm, d // 2), jnp.uint32)
    xs_sorted = _dispatch(dest_p, xpk_p, xs_sorted, tile=tile_p)
    xs_sorted = _dispatch(dest_s, xpk_s, xs_sorted, tile=tile_s)
    ys = _experts(tile_expert, tile_valid, xs_sorted, w_gate, w_up, w_down, tm=tm)
    y_p = _combine(dest_p, h_p, gcol_p, ys, tile=tile_p).reshape(b, s, d)
    y_s = _combine(dest_s, h_s, gcol_s, ys, tile=tile_s).reshape(db, steps, d)

    k4 = lambda z, n: z.reshape(n, -1, N_KV_HEADS, HEAD_DIM)
    nk_p = k4(k_p, b)[:, s - WINDOW:]
    nv_p = k4(v_p, b)[:, s - WINDOW:]
    nk_s = jnp.concatenate([cache_k, k4(k_s, db)], axis=1)[:, steps:]
    nv_s = jnp.concatenate([cache_v, k4(v_s, db)], axis=1)[:, steps:]
    return y_p, y_s, nk_p, nv_p, ut_p[:, 1:], nk_s, nv_s, ut_s[:, 1:]


def kernel(x_prompt, x_sample, cache_k, cache_v, state_pool, attn_norm_w, w_in, q_norm_w, k_norm_w, attn_sinks,
           w_pool, pool_scale, w_out, ffn_norm_w, w_group_router, w_expert_router, w_gate, w_up, w_down):
    depth = w_in.shape[0]
    h_p, h_s = x_prompt, x_sample
    outs = [[] for _ in range(6)]
    for l in range(depth):
        res = _layer(h_p, h_s, cache_k[l], cache_v[l], state_pool[l], attn_norm_w[l], w_in[l], q_norm_w[l],
                     k_norm_w[l], attn_sinks[l], w_pool[l], pool_scale[l], w_out[l], ffn_norm_w[l],
                     w_group_router[l], w_expert_router[l], w_gate[l], w_up[l], w_down[l])
        h_p, h_s = res[0], res[1]
        for acc, r in zip(outs, res[2:]):
            acc.append(r)
    return (h_p, h_s) + tuple(jnp.stack(o, 0) for o in outs)
```
